```python
import math
import jax, jax.numpy as jnp
from jax import lax
import numpy as np

D_MODEL = 1024
BATCH = 32
SEQ = 2048
DEPTH = 1

D_MIX = D_MODEL
GLA_HEADS = 4
GLA_DV = (D_MIX // 2) // GLA_HEADS
GLA_DK = GLA_DV // 2
GLA_GATE_RANK = 16
GLA_TAU = 16.0
GLA_CHUNK = 64
MOBA_HEADS = 8
MOBA_HD = (D_MIX - GLA_HEADS * GLA_DV) // MOBA_HEADS
MOBA_BLOCK = 256
MOBA_TOPK = 3
MOBA_QCHUNK = 8
REL_BUCKETS = 32
REL_MAX_DIST = 128
D_FF = 4 * D_MODEL
N_ADA = 6
EPS = 1e-6

GLA_QK_W = GLA_HEADS * GLA_DK
GLA_V_W = GLA_HEADS * GLA_DV
MOBA_W = MOBA_HEADS * MOBA_HD
IN_WIDTHS = (GLA_QK_W, GLA_QK_W, GLA_V_W, GLA_V_W, GLA_GATE_RANK, MOBA_W, MOBA_W, MOBA_W)
D_IN = sum(IN_WIDTHS)
IN_SPLITS = [int(v) for v in np.cumsum(IN_WIDTHS)[:-1]]

kernel_name = "hybrid_gla_moba_adaln_block"


def rmsnorm(x, g):
    xf = x.astype(jnp.float32)
    y = xf * lax.rsqrt(jnp.mean(xf * xf, axis=-1, keepdims=True) + EPS)
    return y.astype(x.dtype) * g


def modulate(h, shift, scale):
    return h * (1.0 + scale[:, None, :]) + shift[:, None, :]


def gla_mixer(q, k, v, log_a):
    B, S, H, dk = q.shape
    dv = v.shape[-1]
    C = GLA_CHUNK
    N = S // C
    f32 = jnp.float32

    def chunks(t):
        return t.astype(f32).reshape(B, N, C, H, t.shape[-1]).transpose(0, 3, 1, 2, 4)

    q = chunks(q) * (dk ** -0.5)
    k = chunks(k)
    v = chunks(v)
    g = chunks(log_a)
    b = jnp.cumsum(g, axis=3)
    b_last = b[:, :, :, -1:, :]
    b_ref = b[:, :, :, C // 2 - 1:C // 2, :]
    att = jnp.einsum('bhnid,bhnjd->bhnij', q * jnp.exp(b - b_ref), k * jnp.exp(b_ref - b))
    causal = jnp.tril(jnp.ones((C, C), dtype=bool))
    att = jnp.where(causal, att, 0.0)
    o_intra = jnp.einsum('bhnij,bhnjv->bhniv', att, v)
    kv = jnp.einsum('bhncd,bhncv->bhndv', k * jnp.exp(b_last - b), v)
    decay = jnp.exp(b_last[:, :, :, 0, :])

    def step(s_prev, inp):
        kv_n, dec_n = inp
        return dec_n[..., None] * s_prev + kv_n, s_prev

    s0 = jnp.zeros((B, H, dk, dv), f32)
    _, s_before = lax.scan(step, s0, (kv.transpose(2, 0, 1, 3, 4), decay.transpose(2, 0, 1, 3)))
    s_before = s_before.transpose(1, 2, 0, 3, 4)
    o_inter = jnp.einsum('bhncd,bhndv->bhncv', q * jnp.exp(b), s_before)
    o = o_intra + o_inter
    return o.transpose(0, 2, 3, 1, 4).reshape(B, S, H, dv)


def t5_bucket(rel):
    n = jnp.maximum(rel, 0)
    max_exact = REL_BUCKETS // 2
    ratio = jnp.maximum(n, 1).astype(jnp.float32) / max_exact
    large = max_exact + (jnp.log(ratio) / math.log(REL_MAX_DIST / max_exact)
                         * (REL_BUCKETS - max_exact)).astype(jnp.int32)
    large = jnp.minimum(large, REL_BUCKETS - 1)
    return jnp.where(n < max_exact, n, large)


def moba_mixer(q, k, v, rel_bias):
    B, S, H, hd = q.shape
    BLK = MOBA_BLOCK
    QC = MOBA_QCHUNK
    NB = -(-S // BLK)
    Sp = NB * BLK
    topk = min(MOBA_TOPK, NB)
    f32 = jnp.float32
    qh = q.transpose(0, 2, 1, 3) * (hd ** -0.5)
    pad = ((0, 0), (0, 0), (0, Sp - S), (0, 0))
    kb = jnp.pad(k.transpose(0, 2, 1, 3), pad).reshape(B, H, NB, BLK, hd)
    vb = jnp.pad(v.transpose(0, 2, 1, 3), pad).reshape(B, H, NB, BLK, hd)
    k_mean = jnp.mean(kb.astype(f32), axis=3)
    bias_flat = rel_bias.T.reshape(-1)
    head_off = (jnp.arange(H, dtype=jnp.int32) * REL_BUCKETS)[None, :, None, None]
    blk_pos = jnp.arange(BLK, dtype=jnp.int32)
    blk_ids = jnp.arange(NB, dtype=jnp.int32)
    nq = S // QC
    q_chunks = qh.reshape(B, H, nq, QC, hd).transpose(2, 0, 1, 3, 4)
    gather_blocks = jax.vmap(jax.vmap(lambda kk, ii: kk[ii]))

    def one_chunk(args):
        ci, q_c = args
        t = ci * QC + jnp.arange(QC, dtype=jnp.int32)
        own = (ci * QC) // BLK
        gate = jnp.einsum('bhqd,bhnd->bhqn', q_c.astype(f32), k_mean)
        gate = jnp.where(blk_ids < own, gate, -jnp.inf)
        gval, gidx = lax.top_k(gate, topk)
        valid = jnp.isfinite(gval)
        k_sel = gather_blocks(kb, gidx)
        v_sel = gather_blocks(vb, gidx)
        s_sel = jnp.einsum('bhqd,bhqkld->bhqkl', q_c, k_sel).astype(f32)
        pos_sel = gidx[..., None] * BLK + blk_pos
        rel_sel = t[None, None, :, None, None] - pos_sel
        s_sel = s_sel + bias_flat[head_off[..., None] + t5_bucket(rel_sel)]
        s_sel = jnp.where(valid[..., None], s_sel, -jnp.inf)
        k_own = lax.dynamic_slice_in_dim(kb, own, 1, axis=2)[:, :, 0]
        v_own = lax.dynamic_slice_in_dim(vb, own, 1, axis=2)[:, :, 0]
        s_own = jnp.einsum('bhqd,bhld->bhql', q_c, k_own).astype(f32)
        rel_own = t[:, None] - (own * BLK + blk_pos)[None, :]
        s_own = s_own + bias_flat[head_off + t5_bucket(rel_own)[None, None]]
        s_own = jnp.where(rel_own[None, None] >= 0, s_own, -jnp.inf)
        logits = jnp.concatenate([s_sel.reshape(B, H, QC, topk * BLK), s_own], axis=-1)
        p = jax.nn.softmax(logits, axis=-1)
        p_sel = p[..., :topk * BLK].reshape(B, H, QC, topk, BLK).astype(v.dtype)
        p_own = p[..., topk * BLK:].astype(v.dtype)
        return (jnp.einsum('bhqkl,bhqkld->bhqd', p_sel, v_sel)
                + jnp.einsum('bhql,bhld->bhqd', p_own, v_own))

    out = lax.map(one_chunk, (jnp.arange(nq, dtype=jnp.int32), q_chunks))
    return out.transpose(1, 0, 3, 2, 4).reshape(B, S, H, hd)


def setup_inputs(seed: int = 0) -> dict:
    key = jax.random.key(seed)
    ks = jax.random.split(key, 16)
    f32 = jnp.float32
    nrm = lambda k, shape, s: (jax.random.normal(k, shape, f32) * s)
    gain = lambda k, shape: 1.0 + 0.05 * jax.random.normal(k, shape, f32)
    return {
        "x": nrm(ks[0], (BATCH, SEQ, D_MODEL), 1.0),
        "c": nrm(ks[1], (BATCH, D_MODEL), 1.0),
        "w_ada": nrm(ks[2], (DEPTH, D_MODEL, N_ADA * D_MODEL), D_MODEL ** -0.5),
        "b_ada": nrm(ks[3], (DEPTH, N_ADA * D_MODEL), 0.02),
        "g_mix": gain(ks[4], (DEPTH, D_MODEL)),
        "w_in": nrm(ks[5], (DEPTH, D_MODEL, D_IN), D_MODEL ** -0.5),
        "w_gla_gate": nrm(ks[6], (DEPTH, GLA_GATE_RANK, GLA_QK_W), GLA_GATE_RANK ** -0.5),
        "b_gla_gate": nrm(ks[7], (DEPTH, GLA_QK_W), 0.1),
        "g_gla_out": gain(ks[8], (DEPTH, GLA_V_W)),
        "rel_bias": nrm(ks[9], (REL_BUCKETS, MOBA_HEADS), 0.5),
        "w_out": nrm(ks[10], (DEPTH, D_MIX, D_MODEL), D_MIX ** -0.5),
        "g_mlp": gain(ks[11], (DEPTH, D_MODEL)),
        "w_ff1": nrm(ks[12], (DEPTH, D_MODEL, D_FF), D_MODEL ** -0.5),
        "w_ff2": nrm(ks[13], (DEPTH, D_FF, D_MODEL), D_FF ** -0.5),
        "g_final": gain(ks[14], (D_MODEL,)),
    }


def reference(x, c, w_ada, b_ada, g_mix, w_in, w_gla_gate, b_gla_gate, g_gla_out, rel_bias,
              w_out, g_mlp, w_ff1, w_ff2, g_final):
    B, S, _ = x.shape
    c_act = jax.nn.silu(c)
    for l in range(DEPTH):
        ada = c_act @ w_ada[l] + b_ada[l]
        shift_a, scale_a, gate_a, shift_m, scale_m, gate_m = jnp.split(ada, N_ADA, axis=-1)
        h = modulate(rmsnorm(x, g_mix[l]), shift_a, scale_a)
        proj = h @ w_in[l]
        gq, gk, gv, gr, gg, mq, mk, mv = jnp.split(proj, IN_SPLITS, axis=-1)
        log_a = jax.nn.log_sigmoid((gg @ w_gla_gate[l] + b_gla_gate[l]).astype(jnp.float32)) / GLA_TAU
        o_gla = gla_mixer(gq.reshape(B, S, GLA_HEADS, GLA_DK), gk.reshape(B, S, GLA_HEADS, GLA_DK),
                          gv.reshape(B, S, GLA_HEADS, GLA_DV), log_a.reshape(B, S, GLA_HEADS, GLA_DK))
        o_gla = rmsnorm(o_gla.astype(x.dtype), g_gla_out[l].reshape(GLA_HEADS, GLA_DV))
        o_gla = o_gla.reshape(B, S, GLA_V_W) * jax.nn.silu(gr)
        o_moba = moba_mixer(mq.reshape(B, S, MOBA_HEADS, MOBA_HD), mk.reshape(B, S, MOBA_HEADS, MOBA_HD),
                            mv.reshape(B, S, MOBA_HEADS, MOBA_HD), rel_bias).reshape(B, S, MOBA_W)
        y = jnp.concatenate([o_gla, o_moba], axis=-1) @ w_out[l]
        x = x + gate_a[:, None, :] * y
        h = modulate(rmsnorm(x, g_mlp[l]), shift_m, scale_m)
        x = x + gate_m[:, None, :] * (jnp.square(jax.nn.relu(h @ w_ff1[l])) @ w_ff2[l])
    return rmsnorm(x, g_final)
```

```python
import functools
import math

import numpy as np
import jax
import jax.numpy as jnp
from jax import lax
from jax.experimental import pallas as pl
from jax.experimental.pallas import tpu as pltpu

F32, BF16 = jnp.float32, jnp.bfloat16

GLA_HEADS = 4
GLA_GATE_RANK = 16
GLA_TAU = 16.0
GLA_CHUNK = 64
MOBA_HEADS = 8
MOBA_BLOCK = 256
MOBA_TOPK = 3
REL_BUCKETS = 32
REL_MAX_DIST = 128
N_ADA = 6
EPS = 1e-6

V7X_LANES = 128
V7X_VMEM_BYTES = 64 * 2 ** 20
MASKED = -1e30

ROW_TILE = 512
FF_CHUNK = 1024


def _vmem_limit(nbytes):
    return int(min(nbytes * 3 // 2, V7X_VMEM_BYTES * 15 // 16))


def _resident(shape, index_map):
    return pl.BlockSpec(shape, index_map, pipeline_mode=pl.Buffered(1))


def _nt(a, b):
    return lax.dot_general(a, b, (((1,), (1,)), ((), ())), preferred_element_type=F32)


def _tn(a, b):
    return lax.dot_general(a, b, (((0,), (0,)), ((), ())), preferred_element_type=F32)


def _split3(x):
    hi = x.astype(BF16)
    r1 = x - hi.astype(F32)
    mid = r1.astype(BF16)
    lo = (r1 - mid.astype(F32)).astype(BF16)
    return hi, mid, lo


def _ada_kernel(c_ref, w_ref, b_ref, mul_ref, add_ref, o_ref):
    c = c_ref[...]
    c_act = c * jax.nn.sigmoid(c)
    ada = jnp.dot(c_act, w_ref[...], preferred_element_type=F32, precision=lax.Precision.HIGHEST) + b_ref[...]
    o_ref[...] = ada * mul_ref[0] + add_ref[0]


def _ada(c, w_ada, b_ada, mul, add):
    B, D = c.shape
    return pl.pallas_call(
        _ada_kernel,
        out_shape=jax.ShapeDtypeStruct((B, N_ADA * D), F32),
        grid=(N_ADA,),
        in_specs=[
            pl.BlockSpec((B, D), lambda j: (0, 0)),
            pl.BlockSpec((D, D), lambda j: (0, j)),
            pl.BlockSpec((1, D), lambda j: (0, j)),
            pl.BlockSpec((1, 1, D), lambda j: (j, 0, 0)),
            pl.BlockSpec((1, 1, D), lambda j: (j, 0, 0)),
        ],
        out_specs=pl.BlockSpec((B, D), lambda j: (0, j)),
        compiler_params=pltpu.CompilerParams(
            dimension_semantics=("parallel",), vmem_limit_bytes=_vmem_limit(2 * D * D * 4 + 8 * B * D * 4)),
        name="ada",
    )(c, w_ada, b_ada, mul, add)


def _t5_bucket_np(rel):
    n = np.maximum(rel, 0)
    max_exact = REL_BUCKETS // 2
    ratio = np.maximum(n, 1).astype(np.float64) / max_exact
    large = max_exact + (np.log(ratio) / math.log(REL_MAX_DIST / max_exact) * (REL_BUCKETS - max_exact)).astype(np.int64)
    large = np.minimum(large, REL_BUCKETS - 1)
    return np.where(n < max_exact, n, large).astype(np.int32)


def _bias_kernel(rb_ref, idx_ref, o_ref):
    h = pl.program_id(0)
    idx = idx_ref[...]
    b_far = rb_ref[REL_BUCKETS - 1, h]
    acc = jnp.full(idx.shape, MASKED, F32)
    for b in range(REL_BUCKETS):
        acc = jnp.where(idx == b, rb_ref[b, h] - b_far, acc)
    o_ref[0] = acc


def _bias_tiles(rel_bias):
    blk = MOBA_BLOCK
    k = np.arange(blk)[:, None]
    q = np.arange(blk)[None, :]
    own = np.where(q >= k, _t5_bucket_np(q - k), -1)
    prev = _t5_bucket_np(blk + q - k)
    idx = jnp.asarray(np.stack([own, prev]).astype(np.int32))
    return pl.pallas_call(
        _bias_kernel,
        out_shape=jax.ShapeDtypeStruct((MOBA_HEADS, 2, blk, blk), F32),
        grid=(MOBA_HEADS,),
        in_specs=[
            pl.BlockSpec(memory_space=pltpu.SMEM),
            pl.BlockSpec((2, blk, blk), lambda h: (0, 0, 0)),
        ],
        out_specs=pl.BlockSpec((1, 2, blk, blk), lambda h: (h, 0, 0, 0)),
        compiler_params=pltpu.CompilerParams(dimension_semantics=("parallel",)),
        name="bias",
    )(rel_bias, idx)


def _proj_kernel(widths, x_ref, mod_ref, w_ref, wvt_ref, wg_ref, bg_ref,
                 gq_ref, gk_ref, ga_ref, gv_ref, gr_ref, mq_ref, mk_ref, mvt_ref):
    x = x_ref[0]
    shift = mod_ref[0, 0:1, :]
    gain = mod_ref[0, 1:2, :]
    r = lax.rsqrt(jnp.mean(x * x, axis=-1, keepdims=True) + EPS)
    hb = ((x * r) * gain + shift).astype(BF16)

    outs = (gq_ref, gk_ref, gv_ref, gr_ref, mq_ref, mk_ref)
    off = 0
    for o_ref, wd in zip(outs, widths[:-1]):
        o_ref[0] = jnp.dot(hb, w_ref[:, off:off + wd], preferred_element_type=F32).astype(o_ref.dtype)
        off += wd
    gg = jnp.dot(hb, w_ref[:, off:off + widths[-1]], preferred_element_type=F32).astype(BF16)
    z = jnp.dot(gg, wg_ref[...], preferred_element_type=F32) + bg_ref[...]
    ga_ref[0] = (jnp.minimum(z, 0.0) - jnp.log1p(jnp.exp(-jnp.abs(z)))) * (1.0 / GLA_TAU)
    mvt_ref[0] = _nt(wvt_ref[...], hb).astype(BF16)


def _proj(x, mod, w_tok, widths, w_mvt, w_gate2, b_gate2, tm):
    B, S, D = x.shape
    wq, wk, wv, wr, wmq, wmk, wgg = widths
    wmv = w_mvt.shape[0]
    wtot = w_tok.shape[1]
    tok = lambda w: pl.BlockSpec((1, tm, w), lambda b, i: (b, i, 0))
    est = (2 * tm * D * 4 + wtot * D * 2 + wmv * D * 2
           + 2 * tm * (3 * wq * 4 + (wv + wr + wmq + wmk + wmv) * 2) + 6 * tm * D * 4)
    return pl.pallas_call(
        functools.partial(_proj_kernel, widths),
        out_shape=(
            jax.ShapeDtypeStruct((B, S, wq), F32), jax.ShapeDtypeStruct((B, S, wk), F32),
            jax.ShapeDtypeStruct((B, S, wk), F32),
            jax.ShapeDtypeStruct((B, S, wv), BF16), jax.ShapeDtypeStruct((B, S, wr), BF16),
            jax.ShapeDtypeStruct((B, S, wmq), BF16), jax.ShapeDtypeStruct((B, S, wmk), BF16),
            jax.ShapeDtypeStruct((B, wmv, S), BF16),
        ),
        grid=(B, S // tm),
        in_specs=[
            pl.BlockSpec((1, tm, D), lambda b, i: (b, i, 0)),
            pl.BlockSpec((1, N_ADA, D), lambda b, i: (b, 0, 0)),
            _resident((D, wtot), lambda b, i: (0, 0)),
            _resident((wmv, D), lambda b, i: (0, 0)),
            _resident(w_gate2.shape, lambda b, i: (0, 0)),
            _resident(b_gate2.shape, lambda b, i: (0, 0)),
        ],
        out_specs=(tok(wq), tok(wk), tok(wk), tok(wv), tok(wr), tok(wmq), tok(wmk),
                   pl.BlockSpec((1, wmv, tm), lambda b, i: (b, 0, i))),
        compiler_params=pltpu.CompilerParams(
            dimension_semantics=("parallel", "parallel"), vmem_limit_bytes=_vmem_limit(est)),
        name="proj",
    )(x, mod, w_tok, w_mvt, w_gate2, b_gate2)


def _gla_kernel(dk, dv, q_ref, k_ref, a_ref, v_ref, r_ref, g_ref, o_ref, st_ref):
    S = q_ref.shape[1]
    C = GLA_CHUNK
    N = S // C
    W = 2 * dk
    q = q_ref[0].reshape(N, C, W)
    k = k_ref[0].reshape(N, C, W)
    g = a_ref[0].reshape(N, C, W)

    row = lax.broadcasted_iota(jnp.int32, (C, C), 0)
    col = lax.broadcasted_iota(jnp.int32, (C, C), 1)
    causal = col <= row
    tri = jnp.broadcast_to(causal.astype(BF16)[None], (N, C, C))
    b = None
    for part in _split3(g):
        t = jnp.einsum('nij,njd->nid', tri, part, preferred_element_type=F32)
        b = t if b is None else b + t
    b_last = b[:, C - 1:C, :]
    b_mid = b[:, C // 2 - 1:C // 2, :]

    qd = q * jnp.exp(b - b_mid)
    kd = (k * jnp.exp(b_mid - b)).astype(BF16)
    kdec = (k * jnp.exp(b_last - b)).astype(BF16)
    qb = q * jnp.exp(b)
    decay = jnp.exp(b_last)

    lane = lax.broadcasted_iota(jnp.int32, (1, 1, W), 2)
    for hh in range(2):
        mine = (lane // dk) == hh
        v = v_ref[0, :, hh * dv:(hh + 1) * dv].reshape(N, C, dv)
        qd_h = jnp.where(mine, qd, 0.0).astype(BF16)
        qb_h = jnp.where(mine, qb, 0.0).astype(BF16)
        att = jnp.einsum('nid,njd->nij', qd_h, kd, preferred_element_type=F32)
        att = jnp.where(causal[None], att, 0.0).astype(BF16)
        o = jnp.einsum('nij,njv->niv', att, v, preferred_element_type=F32)
        kvt = jnp.einsum('ncv,ncd->nvd', v, kdec, preferred_element_type=F32)
        state = jnp.zeros((dv, W), F32)
        for n in range(N):
            st_ref[n] = state.astype(BF16)
            state = state * decay[n] + kvt[n]
        o = o + jnp.einsum('ncd,nvd->ncv', qb_h, st_ref[...], preferred_element_type=F32)
        o = o.reshape(S, dv)
        rr = lax.rsqrt(jnp.mean(o * o, axis=-1, keepdims=True) + EPS)
        gate = r_ref[0, :, hh * dv:(hh + 1) * dv].astype(F32)
        gate = gate * jax.nn.sigmoid(gate)
        o_ref[0, :, hh * dv:(hh + 1) * dv] = ((o * rr) * g_ref[:, hh * dv:(hh + 1) * dv] * gate).astype(BF16)


def _gla(gq, gk, ga, gv, gr, g_out, dk, dv):
    B, S, _ = gq.shape
    pairs = GLA_HEADS // 2
    W = 2 * dk
    N = S // GLA_CHUNK
    qspec = pl.BlockSpec((1, S, W), lambda b, p: (b, 0, p))
    vspec = pl.BlockSpec((1, S, 2 * dv), lambda b, p: (b, 0, p))
    est = 2 * (3 * S * W * 4 + 3 * S * 2 * dv * 2) + 10 * S * W * 4 + 4 * N * dv * W * 4 + 4 * S * dv * 4
    return pl.pallas_call(
        functools.partial(_gla_kernel, dk, dv),
        out_shape=jax.ShapeDtypeStruct((B, S, GLA_HEADS * dv), BF16),
        grid=(B, pairs),
        in_specs=[qspec, qspec, qspec, vspec, vspec, pl.BlockSpec((1, 2 * dv), lambda b, p: (0, p))],
        out_specs=vspec,
        scratch_shapes=[pltpu.VMEM((N, dv, W), BF16)],
        compiler_params=pltpu.CompilerParams(
            dimension_semantics=("parallel", "parallel"), vmem_limit_bytes=_vmem_limit(est)),
        name="gla",
    )(gq, gk, ga, gv, gr, g_out)


def _moba_kernel(hd, q_ref, k_ref, vt_ref, t_ref, o_ref, s_ref, p_ref):
    S = q_ref.shape[1]
    BLK = MOBA_BLOCK
    NB = S // BLK
    W = 2 * hd
    hh = pl.program_id(1) % 2
    lane = lax.broadcasted_iota(jnp.int32, (1, W), 1)
    mine = (lane // hd) == hh
    k2 = k_ref[0]
    q2 = jnp.where(mine, q_ref[0], jnp.zeros((), BF16))

    kmean = jnp.sum(k2.astype(F32).reshape(NB, BLK, W), axis=1) * (1.0 / BLK)
    km_hi = kmean.astype(BF16)
    km_lo = (kmean - km_hi.astype(F32)).astype(BF16)
    gate = _nt(km_hi, q2) + _nt(km_lo, q2)
    blk = lax.broadcasted_iota(jnp.int32, (NB, S), 0)
    own = lax.broadcasted_iota(jnp.int32, (NB, S), 1) // BLK
    valid = blk < own
    rank = jnp.zeros((NB, S), jnp.int32)
    for m in range(NB):
        gm = gate[m:m + 1, :]
        beats = (gm > gate) | ((gm == gate) & (m < blk))
        rank = rank + (valid[m:m + 1, :] & beats).astype(jnp.int32)
    chosen = valid & (rank < min(MOBA_TOPK, NB))
    maskt = jnp.where(chosen, 0.0, MASKED)

    for i in range(NB):
        qi = q2[i * BLK:(i + 1) * BLK]
        m_run = None
        for j in range(i, -1, -1):
            s = _nt(k2[j * BLK:(j + 1) * BLK], qi)
            if j == i:
                s = s + t_ref[0, 0]
            elif j == i - 1:
                s = s + t_ref[0, 1]
            s_ref[j * BLK:(j + 1) * BLK, :] = s
            mb = jnp.max(s, axis=0, keepdims=True)
            if j < i:
                mb = mb + maskt[j:j + 1, i * BLK:(i + 1) * BLK]
            m_run = mb if m_run is None else jnp.maximum(m_run, mb)
        l_run = jnp.zeros((1, BLK), F32)
        for j in range(i + 1):
            c = m_run if j == i else m_run - maskt[j:j + 1, i * BLK:(i + 1) * BLK]
            p = jnp.exp(s_ref[j * BLK:(j + 1) * BLK, :] - c)
            l_run = l_run + jnp.sum(p, axis=0, keepdims=True)
            p_ref[j * BLK:(j + 1) * BLK, :] = p.astype(BF16)
        nk = (i + 1) * BLK
        ot = jnp.dot(vt_ref[0, :, :nk], p_ref[:nk, :], preferred_element_type=F32)
        o_ref[0, :, i * BLK:(i + 1) * BLK] = (ot / l_run).astype(BF16)


def _moba(mq, mk, mvt, tiles, hd):
    B, S, _ = mq.shape
    W = 2 * hd
    BLK = MOBA_BLOCK
    est = 2 * (2 * S * W * 2 + 2 * hd * S * 2 + 2 * BLK * BLK * 4) + S * BLK * 6 + 3 * S * W * 4 + 16 * BLK * BLK * 4
    return pl.pallas_call(
        functools.partial(_moba_kernel, hd),
        out_shape=jax.ShapeDtypeStruct((B, MOBA_HEADS * hd, S), BF16),
        grid=(B, MOBA_HEADS),
        in_specs=[
            pl.BlockSpec((1, S, W), lambda b, h: (b, 0, h // 2)),
            pl.BlockSpec((1, S, W), lambda b, h: (b, 0, h // 2)),
            pl.BlockSpec((1, hd, S), lambda b, h: (b, h, 0)),
            pl.BlockSpec((1, 2, BLK, BLK), lambda b, h: (h, 0, 0, 0)),
        ],
        out_specs=pl.BlockSpec((1, hd, S), lambda b, h: (b, h, 0)),
        scratch_shapes=[pltpu.VMEM((S, BLK), F32), pltpu.VMEM((S, BLK), BF16)],
        compiler_params=pltpu.CompilerParams(
            dimension_semantics=("parallel", "parallel"), vmem_limit_bytes=_vmem_limit(est)),
        name="moba",
    )(mq, mk, mvt, tiles)


def _mlp_kernel(x_ref, og_ref, omt_ref, mod_ref, woa_ref, wob_ref, w1_ref, w2_ref, gf_ref, o_ref):
    gate_a = mod_ref[0, 2:3, :]
    shift_m = mod_ref[0, 3:4, :]
    gain_m = mod_ref[0, 4:5, :]
    gate_m = mod_ref[0, 5:6, :]
    y = jnp.dot(og_ref[0], woa_ref[...], preferred_element_type=F32) + _tn(omt_ref[0], wob_ref[...])
    x1 = x_ref[0] + gate_a * y
    r = lax.rsqrt(jnp.mean(x1 * x1, axis=-1, keepdims=True) + EPS)
    hb = ((x1 * r) * gain_m + shift_m).astype(BF16)
    ff = None
    for c0 in range(0, w1_ref.shape[1], FF_CHUNK):
        u = jnp.dot(hb, w1_ref[:, c0:c0 + FF_CHUNK], preferred_element_type=F32)
        u = jnp.maximum(u, 0.0)
        t = jnp.dot((u * u).astype(BF16), w2_ref[c0:c0 + FF_CHUNK, :], preferred_element_type=F32)
        ff = t if ff is None else ff + t
    x2 = x1 + gate_m * ff
    r2 = lax.rsqrt(jnp.mean(x2 * x2, axis=-1, keepdims=True) + EPS)
    o_ref[0] = (x2 * r2) * gf_ref[...]


def _mlp(x, o_gla, o_mobat, mod, w_out_a, w_out_b, w1, w2, g_final, tm):
    B, S, D = x.shape
    wa, wb, dff = w_out_a.shape[0], w_out_b.shape[0], w1.shape[1]
    est = ((wa + wb) * D * 2 + 2 * D * dff * 2 + 4 * tm * D * 4 + 2 * tm * (wa + wb) * 2
           + 6 * tm * D * 4 + 2 * tm * FF_CHUNK * 4)
    return pl.pallas_call(
        _mlp_kernel,
        out_shape=jax.ShapeDtypeStruct((B, S, D), x.dtype),
        grid=(B, S // tm),
        in_specs=[
            pl.BlockSpec((1, tm, D), lambda b, i: (b, i, 0)),
            pl.BlockSpec((1, tm, wa), lambda b, i: (b, i, 0)),
            pl.BlockSpec((1, wb, tm), lambda b, i: (b, 0, i)),
            pl.BlockSpec((1, N_ADA, D), lambda b, i: (b, 0, 0)),
            _resident((wa, D), lambda b, i: (0, 0)),
            _resident((wb, D), lambda b, i: (0, 0)),
            _resident((D, dff), lambda b, i: (0, 0)),
            _resident((dff, D), lambda b, i: (0, 0)),
            _resident((1, D), lambda b, i: (0, 0)),
        ],
        out_specs=pl.BlockSpec((1, tm, D), lambda b, i: (b, i, 0)),
        compiler_params=pltpu.CompilerParams(
            dimension_semantics=("parallel", "parallel"), vmem_limit_bytes=_vmem_limit(est)),
        name="mlp",
    )(x, o_gla, o_mobat, mod, w_out_a, w_out_b, w1, w2, g_final)


def _layer(x, mod, w_in, w_gate, b_gate, g_gla_out, tiles, w_out, w_ff1, w_ff2, g_last):
    B, S, D = x.shape
    dv = (D // 2) // GLA_HEADS
    dk = dv // 2
    qk_w, v_w = GLA_HEADS * dk, GLA_HEADS * dv
    hd = (D - v_w) // MOBA_HEADS
    m_w = MOBA_HEADS * hd
    assert 2 * dk == V7X_LANES and dv == V7X_LANES and 2 * hd == V7X_LANES, "head pairs must fill one lane group"
    assert S % MOBA_BLOCK == 0 and S % GLA_CHUNK == 0
    tm = min(ROW_TILE, S)
    assert S % tm == 0

    bounds = np.cumsum([0, qk_w, qk_w, v_w, v_w, GLA_GATE_RANK, m_w, m_w, m_w])
    seg = [w_in[:, bounds[i]:bounds[i + 1]] for i in range(8)]
    gq_w, gk_w, gv_w, gr_w, gg_w, mq_w, mk_w, mv_w = seg
    gg_pad = jnp.pad(gg_w, ((0, 0), (0, V7X_LANES - GLA_GATE_RANK)))
    w_tok = jnp.concatenate([gq_w * dk ** -0.5, gk_w, gv_w, gr_w, mq_w * hd ** -0.5, mk_w, gg_pad], axis=1).astype(BF16)
    widths = (qk_w, qk_w, v_w, v_w, m_w, m_w, V7X_LANES)
    w_mvt = mv_w.T.astype(BF16)
    w_gate2 = jnp.pad(w_gate, ((0, V7X_LANES - GLA_GATE_RANK), (0, 0))).astype(BF16)

    gq, gk, ga, gv, gr, mq, mk, mvt = _proj(x, mod, w_tok, widths, w_mvt, w_gate2, b_gate[None, :], tm)
    o_gla = _gla(gq, gk, ga, gv, gr, g_gla_out[None, :], dk, dv)
    o_mobat = _moba(mq, mk, mvt, tiles, hd)
    w_out_b16 = w_out.astype(BF16)
    return _mlp(x, o_gla, o_mobat, mod, w_out_b16[:v_w], w_out_b16[v_w:], w_ff1.astype(BF16), w_ff2.astype(BF16),
                g_last[None, :], tm)


def kernel(x, c, w_ada, b_ada, g_mix, w_in, w_gla_gate, b_gla_gate, g_gla_out, rel_bias, w_out, g_mlp, w_ff1, w_ff2,
           g_final):
    depth = w_ada.shape[0]
    assert depth == 1, "the fused MLP kernel applies the final RMSNorm; one layer is supported"
    D = x.shape[-1]
    tiles = _bias_tiles(rel_bias)
    l = 0
    ones, zeros = jnp.ones((D,), F32), jnp.zeros((D,), F32)
    mul = jnp.stack([ones, g_mix[l], ones, ones, g_mlp[l], ones])[:, None, :]
    add = jnp.stack([zeros, g_mix[l], zeros, zeros, g_mlp[l], zeros])[:, None, :]
    mod = _ada(c, w_ada[l], b_ada[l][None, :], mul, add).reshape(c.shape[0], N_ADA, D)
    return _layer(x, mod, w_in[l], w_gla_gate[l], b_gla_gate[l], g_gla_out[l], tiles, w_out[l], w_ff1[l], w_ff2[l],
                  g_final)
```

```python
import functools
import math

import numpy as np
import jax
import jax.numpy as jnp
from jax import lax
from jax.experimental import pallas as pl
from jax.experimental.pallas import tpu as pltpu

F32, BF16 = jnp.float32, jnp.bfloat16

GLA_HEADS = 4
GLA_GATE_RANK = 16
GLA_TAU = 16.0
GLA_CHUNK = 64
MOBA_HEADS = 8
MOBA_BLOCK = 256
MOBA_TOPK = 3
REL_BUCKETS = 32
REL_MAX_DIST = 128
N_ADA = 6
EPS = 1e-6

V7X_LANES = 128
V7X_VMEM_BYTES = 64 * 2 ** 20
MASKED = -1e30
LOG2E = math.log2(math.e)

ROW_TILE = 512
FF_CHUNK = 1024


def _vmem_limit(nbytes):
    return int(min(nbytes * 3 // 2, V7X_VMEM_BYTES * 15 // 16))


def _resident(shape, index_map):
    return pl.BlockSpec(shape, index_map, pipeline_mode=pl.Buffered(1))


def _nt(a, b):
    return lax.dot_general(a, b, (((1,), (1,)), ((), ())), preferred_element_type=F32)


def _tn(a, b):
    return lax.dot_general(a, b, (((0,), (0,)), ((), ())), preferred_element_type=F32)


def _split3(x):
    hi = x.astype(BF16)
    r1 = x - hi.astype(F32)
    mid = r1.astype(BF16)
    lo = (r1 - mid.astype(F32)).astype(BF16)
    return hi, mid, lo


def _ada_kernel(c_ref, w_ref, b_ref, mul_ref, add_ref, o_ref):
    c = c_ref[...]
    c_act = c * jax.nn.sigmoid(c)
    ada = jnp.dot(c_act, w_ref[...], preferred_element_type=F32, precision=lax.Precision.HIGHEST) + b_ref[...]
    o_ref[...] = ada * mul_ref[0] + add_ref[0]


def _ada(c, w_ada, b_ada, mul, add):
    B, D = c.shape
    return pl.pallas_call(
        _ada_kernel,
        out_shape=jax.ShapeDtypeStruct((B, N_ADA * D), F32),
        grid=(N_ADA,),
        in_specs=[
            pl.BlockSpec((B, D), lambda j: (0, 0)),
            pl.BlockSpec((D, D), lambda j: (0, j)),
            pl.BlockSpec((1, D), lambda j: (0, j)),
            pl.BlockSpec((1, 1, D), lambda j: (j, 0, 0)),
            pl.BlockSpec((1, 1, D), lambda j: (j, 0, 0)),
        ],
        out_specs=pl.BlockSpec((B, D), lambda j: (0, j)),
        compiler_params=pltpu.CompilerParams(
            dimension_semantics=("parallel",), vmem_limit_bytes=_vmem_limit(2 * D * D * 4 + 8 * B * D * 4)),
        name="ada",
    )(c, w_ada, b_ada, mul, add)


def _t5_bucket_np(rel):
    n = np.maximum(rel, 0)
    max_exact = REL_BUCKETS // 2
    ratio = np.maximum(n, 1).astype(np.float64) / max_exact
    large = max_exact + (np.log(ratio) / math.log(REL_MAX_DIST / max_exact) * (REL_BUCKETS - max_exact)).astype(np.int64)
    large = np.minimum(large, REL_BUCKETS - 1)
    return np.where(n < max_exact, n, large).astype(np.int32)


def _bias_kernel(rb_ref, idx_ref, o_ref):
    blk = idx_ref.shape[-1]
    idx = idx_ref[...]
    for hh in range(2):
        h = 2 * pl.program_id(0) + hh
        b_far = rb_ref[REL_BUCKETS - 1, h]
        acc = jnp.full(idx.shape, MASKED, F32)
        for b in range(REL_BUCKETS):
            acc = jnp.where(idx == b, (rb_ref[b, h] - b_far) * LOG2E, acc)
        o_ref[0, :, :, hh * blk:(hh + 1) * blk] = acc


def _bias_tiles(rel_bias):
    blk = MOBA_BLOCK
    k = np.arange(blk)[:, None]
    q = np.arange(blk)[None, :]
    own = np.where(q >= k, _t5_bucket_np(q - k), -1)
    prev = _t5_bucket_np(blk + q - k)
    idx = jnp.asarray(np.stack([own, prev]).astype(np.int32))
    return pl.pallas_call(
        _bias_kernel,
        out_shape=jax.ShapeDtypeStruct((MOBA_HEADS // 2, 2, blk, 2 * blk), F32),
        grid=(MOBA_HEADS // 2,),
        in_specs=[
            pl.BlockSpec(memory_space=pltpu.SMEM),
            pl.BlockSpec((2, blk, blk), lambda p: (0, 0, 0)),
        ],
        out_specs=pl.BlockSpec((1, 2, blk, 2 * blk), lambda p: (p, 0, 0, 0)),
        compiler_params=pltpu.CompilerParams(dimension_semantics=("parallel",)),
        name="bias",
    )(rel_bias, idx)


def _proj_kernel(widths, x_ref, mod_ref, w_ref, wvt_ref, wg_ref, bg_ref,
                 gq_ref, gk_ref, ga_ref, gv_ref, gr_ref, mq_ref, mk_ref, mvt_ref):
    x = x_ref[0]
    shift = mod_ref[0, 0:1, :]
    gain = mod_ref[0, 1:2, :]
    r = lax.rsqrt(jnp.mean(x * x, axis=-1, keepdims=True) + EPS)
    hb = ((x * r) * gain + shift).astype(BF16)

    outs = (gq_ref, gk_ref, gv_ref, gr_ref, mq_ref, mk_ref)
    off = 0
    for o_ref, wd in zip(outs, widths[:-1]):
        o_ref[0] = jnp.dot(hb, w_ref[:, off:off + wd], preferred_element_type=F32).astype(o_ref.dtype)
        off += wd
    gg = jnp.dot(hb, w_ref[:, off:off + widths[-1]], preferred_element_type=F32).astype(BF16)
    z = jnp.dot(gg, wg_ref[...], preferred_element_type=F32) + bg_ref[...]
    ga_ref[0] = (jnp.minimum(z, 0.0) - jnp.log1p(jnp.exp(-jnp.abs(z)))) * (1.0 / GLA_TAU)
    mvt_ref[0] = _nt(wvt_ref[...], hb).astype(BF16)


def _proj(x, mod, w_tok, widths, w_mvt, w_gate2, b_gate2, tm):
    B, S, D = x.shape
    wq, wk, wv, wr, wmq, wmk, wgg = widths
    wmv = w_mvt.shape[0]
    wtot = w_tok.shape[1]
    tok = lambda w: pl.BlockSpec((1, tm, w), lambda b, i: (b, i, 0))
    est = (2 * tm * D * 4 + wtot * D * 2 + wmv * D * 2
           + 2 * tm * (3 * wq * 4 + (wv + wr + wmq + wmk + wmv) * 2) + 6 * tm * D * 4)
    return pl.pallas_call(
        functools.partial(_proj_kernel, widths),
        out_shape=(
            jax.ShapeDtypeStruct((B, S, wq), F32), jax.ShapeDtypeStruct((B, S, wk), F32),
            jax.ShapeDtypeStruct((B, S, wk), F32),
            jax.ShapeDtypeStruct((B, S, wv), BF16), jax.ShapeDtypeStruct((B, S, wr), BF16),
            jax.ShapeDtypeStruct((B, S, wmq), BF16), jax.ShapeDtypeStruct((B, S, wmk), BF16),
            jax.ShapeDtypeStruct((B, wmv, S), BF16),
        ),
        grid=(B, S // tm),
        in_specs=[
            pl.BlockSpec((1, tm, D), lambda b, i: (b, i, 0)),
            pl.BlockSpec((1, N_ADA, D), lambda b, i: (b, 0, 0)),
            _resident((D, wtot), lambda b, i: (0, 0)),
            _resident((wmv, D), lambda b, i: (0, 0)),
            _resident(w_gate2.shape, lambda b, i: (0, 0)),
            _resident(b_gate2.shape, lambda b, i: (0, 0)),
        ],
        out_specs=(tok(wq), tok(wk), tok(wk), tok(wv), tok(wr), tok(wmq), tok(wmk),
                   pl.BlockSpec((1, wmv, tm), lambda b, i: (b, 0, i))),
        compiler_params=pltpu.CompilerParams(
            dimension_semantics=("parallel", "parallel"), vmem_limit_bytes=_vmem_limit(est)),
        name="proj",
    )(x, mod, w_tok, w_mvt, w_gate2, b_gate2)


def _gla_kernel(dk, dv, q_ref, k_ref, a_ref, v_ref, r_ref, g_ref, o_ref, st_ref):
    S = q_ref.shape[1]
    C = GLA_CHUNK
    N = S // C
    W = 2 * dk
    q = q_ref[0].reshape(N, C, W)
    k = k_ref[0].reshape(N, C, W)
    g = a_ref[0].reshape(N, C, W)

    row = lax.broadcasted_iota(jnp.int32, (C, C), 0)
    col = lax.broadcasted_iota(jnp.int32, (C, C), 1)
    causal = col <= row
    tri = jnp.broadcast_to(causal.astype(BF16)[None], (N, C, C))
    b = None
    for part in _split3(g):
        t = jnp.einsum('nij,njd->nid', tri, part, preferred_element_type=F32)
        b = t if b is None else b + t
    b_last = b[:, C - 1:C, :]
    b_mid = b[:, C // 2 - 1:C // 2, :]

    qd = q * jnp.exp(b - b_mid)
    kd = (k * jnp.exp(b_mid - b)).astype(BF16)
    kdec = (k * jnp.exp(b_last - b)).astype(BF16)
    qb = q * jnp.exp(b)
    decay = jnp.exp(b_last)

    lane = lax.broadcasted_iota(jnp.int32, (1, 1, W), 2)
    for hh in range(2):
        mine = (lane // dk) == hh
        v = v_ref[0, :, hh * dv:(hh + 1) * dv].reshape(N, C, dv)
        qd_h = jnp.where(mine, qd, 0.0).astype(BF16)
        qb_h = jnp.where(mine, qb, 0.0).astype(BF16)
        att = jnp.einsum('nid,njd->nij', qd_h, kd, preferred_element_type=F32)
        att = jnp.where(causal[None], att, 0.0).astype(BF16)
        o = jnp.einsum('nij,njv->niv', att, v, preferred_element_type=F32)
        kvt = jnp.einsum('ncv,ncd->nvd', v, kdec, preferred_element_type=F32)
        state = jnp.zeros((dv, W), F32)
        for n in range(N):
            st_ref[n] = state.astype(BF16)
            state = state * decay[n] + kvt[n]
        o = o + jnp.einsum('ncd,nvd->ncv', qb_h, st_ref[...], preferred_element_type=F32)
        o = o.reshape(S, dv)
        rr = lax.rsqrt(jnp.mean(o * o, axis=-1, keepdims=True) + EPS)
        gate = r_ref[0, :, hh * dv:(hh + 1) * dv].astype(F32)
        gate = gate * jax.nn.sigmoid(gate)
        o_ref[0, :, hh * dv:(hh + 1) * dv] = ((o * rr) * g_ref[:, hh * dv:(hh + 1) * dv] * gate).astype(BF16)


def _gla(gq, gk, ga, gv, gr, g_out, dk, dv):
    B, S, _ = gq.shape
    pairs = GLA_HEADS // 2
    W = 2 * dk
    N = S // GLA_CHUNK
    qspec = pl.BlockSpec((1, S, W), lambda b, p: (b, 0, p))
    vspec = pl.BlockSpec((1, S, 2 * dv), lambda b, p: (b, 0, p))
    est = 2 * (3 * S * W * 4 + 3 * S * 2 * dv * 2) + 10 * S * W * 4 + 4 * N * dv * W * 4 + 4 * S * dv * 4
    return pl.pallas_call(
        functools.partial(_gla_kernel, dk, dv),
        out_shape=jax.ShapeDtypeStruct((B, S, GLA_HEADS * dv), BF16),
        grid=(B, pairs),
        in_specs=[qspec, qspec, qspec, vspec, vspec, pl.BlockSpec((1, 2 * dv), lambda b, p: (0, p))],
        out_specs=vspec,
        scratch_shapes=[pltpu.VMEM((N, dv, W), BF16)],
        compiler_params=pltpu.CompilerParams(
            dimension_semantics=("parallel", "parallel"), vmem_limit_bytes=_vmem_limit(est)),
        name="gla",
    )(gq, gk, ga, gv, gr, g_out)


def _moba_kernel(hd, q_ref, k_ref, vt_ref, t_ref, o_ref, s_ref, p_ref, ve_ref):
    S = q_ref.shape[1]
    BLK = MOBA_BLOCK
    NB = S // BLK
    W = 2 * hd
    VR = ve_ref.shape[0] // 2
    lane = lax.broadcasted_iota(jnp.int32, (1, W), 1)
    head_lanes = (lane < hd, lane >= hd)
    zero = jnp.zeros((), BF16)
    k2 = k_ref[0]
    q2 = q_ref[0]

    for h in range(2):
        ve_ref[h * VR:h * VR + hd, :] = vt_ref[0, h * hd:(h + 1) * hd, :]
        ve_ref[h * VR + hd:(h + 1) * VR, :] = jnp.ones((VR - hd, S), BF16)

    first = MOBA_TOPK + 1
    masks = None
    if NB > first:
        lo = first * BLK
        kmean = jnp.sum(k2.astype(F32).reshape(NB, BLK, W), axis=1) * (1.0 / BLK)
        parts = []
        for h in range(2):
            km = jnp.where(head_lanes[h], kmean, 0.0)
            km_hi = km.astype(BF16).astype(F32)
            parts += [km_hi, km - km_hi]
        g4 = _nt(jnp.concatenate(parts, axis=0).astype(BF16), q2[lo:])
        blk = lax.broadcasted_iota(jnp.int32, (NB, S - lo), 0)
        own = lax.broadcasted_iota(jnp.int32, (NB, S - lo), 1) // BLK + first
        masks = []
        for h in range(2):
            gate = g4[2 * h * NB:(2 * h + 1) * NB] + g4[(2 * h + 1) * NB:(2 * h + 2) * NB]
            rank = jnp.zeros(gate.shape, jnp.int32)
            for r in range(1, NB):
                rival = pltpu.roll(gate, r, axis=0)
                rblk = jnp.where(blk < r, blk + (NB - r), blk - r)
                beats = (rblk < own) & ((rival > gate) | ((rival == gate) & (rblk < blk)))
                rank = rank + beats.astype(jnp.int32)
            chosen = (blk < own) & (rank < MOBA_TOPK)
            masks.append(jnp.where(chosen, 0.0, MASKED))

    def mrow(i, j):
        if j == i or i < first:
            return None
        cols = slice((i - first) * BLK, (i - first + 1) * BLK)
        return jnp.concatenate([masks[h][j:j + 1, cols] for h in range(2)], axis=1)

    qtile = {}

    def scores(i, j):
        if i not in qtile:
            qi = q2[i * BLK:(i + 1) * BLK]
            qtile[i] = jnp.concatenate([jnp.where(head_lanes[h], qi, zero) for h in range(2)], axis=0)
        s = _nt(k2[j * BLK:(j + 1) * BLK], qtile[i])
        if j == i:
            s = s + t_ref[0, 0]
        elif j == i - 1:
            s = s + t_ref[0, 1]
        s_ref[i % 2, j * BLK:(j + 1) * BLK, :] = s
        mb = jnp.max(s, axis=0, keepdims=True)
        m = mrow(i, j)
        return mb if m is None else mb + m

    def weights(i, j, m_run):
        m = mrow(i, j)
        c = m_run if m is None else m_run - m
        rows = slice(j * BLK, (j + 1) * BLK)
        p_ref[i % 2, rows, :] = jnp.exp2(s_ref[i % 2, rows, :] - c).astype(BF16)

    col_max = {0: [scores(0, 0)]}
    for i in range(NB):
        m_run = functools.reduce(jnp.maximum, col_max.pop(i))
        nxt = list(range(i + 1, -1, -1)) if i + 1 < NB else []
        col_max[i + 1] = []
        for j in range(i + 1):
            weights(i, j, m_run)
            while nxt and len(col_max[i + 1]) * (i + 1) < (j + 1) * (i + 2):
                col_max[i + 1].append(scores(i + 1, nxt.pop(0)))
        nk = (i + 1) * BLK
        for h in range(2):
            ot = jnp.dot(ve_ref[h * VR:(h + 1) * VR, :nk], p_ref[i % 2, :nk, h * BLK:(h + 1) * BLK],
                         preferred_element_type=F32)
            o_ref[0, h * hd:(h + 1) * hd, i * BLK:(i + 1) * BLK] = (ot[:hd] / ot[hd:hd + 1]).astype(BF16)


def _moba(mq, mk, mvt, tiles, hd):
    B, S, _ = mq.shape
    W = 2 * hd
    BLK = MOBA_BLOCK
    assert S // BLK <= 8, "block ranking keeps the candidate blocks on the sublanes of one vector register"
    ve_rows = 2 * (hd + 16)
    est = (2 * (2 * S * W * 2 + 2 * W * S * 2 + 4 * BLK * BLK * 4) + 2 * S * 2 * BLK * 6 + ve_rows * S * 2
           + 3 * S * W * 4 + 32 * BLK * BLK * 4)
    return pl.pallas_call(
        functools.partial(_moba_kernel, hd),
        out_shape=jax.ShapeDtypeStruct((B, MOBA_HEADS * hd, S), BF16),
        grid=(B, MOBA_HEADS // 2),
        in_specs=[
            pl.BlockSpec((1, S, W), lambda b, p: (b, 0, p)),
            pl.BlockSpec((1, S, W), lambda b, p: (b, 0, p)),
            pl.BlockSpec((1, W, S), lambda b, p: (b, p, 0)),
            pl.BlockSpec((1, 2, BLK, 2 * BLK), lambda b, p: (p, 0, 0, 0)),
        ],
        out_specs=pl.BlockSpec((1, W, S), lambda b, p: (b, p, 0)),
        scratch_shapes=[pltpu.VMEM((2, S, 2 * BLK), F32), pltpu.VMEM((2, S, 2 * BLK), BF16),
                        pltpu.VMEM((ve_rows, S), BF16)],
        compiler_params=pltpu.CompilerParams(
            dimension_semantics=("parallel", "parallel"), vmem_limit_bytes=_vmem_limit(est)),
        name="moba",
    )(mq, mk, mvt, tiles)


def _mlp_kernel(x_ref, og_ref, omt_ref, mod_ref, woa_ref, wob_ref, w1_ref, w2_ref, gf_ref, o_ref):
    gate_a = mod_ref[0, 2:3, :]
    shift_m = mod_ref[0, 3:4, :]
    gain_m = mod_ref[0, 4:5, :]
    gate_m = mod_ref[0, 5:6, :]
    y = jnp.dot(og_ref[0], woa_ref[...], preferred_element_type=F32) + _tn(omt_ref[0], wob_ref[...])
    x1 = x_ref[0] + gate_a * y
    r = lax.rsqrt(jnp.mean(x1 * x1, axis=-1, keepdims=True) + EPS)
    hb = ((x1 * r) * gain_m + shift_m).astype(BF16)
    ff = None
    for c0 in range(0, w1_ref.shape[1], FF_CHUNK):
        u = jnp.dot(hb, w1_ref[:, c0:c0 + FF_CHUNK], preferred_element_type=F32)
        u = jnp.maximum(u, 0.0)
        t = jnp.dot((u * u).astype(BF16), w2_ref[c0:c0 + FF_CHUNK, :], preferred_element_type=F32)
        ff = t if ff is None else ff + t
    x2 = x1 + gate_m * ff
    r2 = lax.rsqrt(jnp.mean(x2 * x2, axis=-1, keepdims=True) + EPS)
    o_ref[0] = (x2 * r2) * gf_ref[...]


def _mlp(x, o_gla, o_mobat, mod, w_out_a, w_out_b, w1, w2, g_final, tm):
    B, S, D = x.shape
    wa, wb, dff = w_out_a.shape[0], w_out_b.shape[0], w1.shape[1]
    est = ((wa + wb) * D * 2 + 2 * D * dff * 2 + 4 * tm * D * 4 + 2 * tm * (wa + wb) * 2
           + 6 * tm * D * 4 + 2 * tm * FF_CHUNK * 4)
    return pl.pallas_call(
        _mlp_kernel,
        out_shape=jax.ShapeDtypeStruct((B, S, D), x.dtype),
        grid=(B, S // tm),
        in_specs=[
            pl.BlockSpec((1, tm, D), lambda b, i: (b, i, 0)),
            pl.BlockSpec((1, tm, wa), lambda b, i: (b, i, 0)),
            pl.BlockSpec((1, wb, tm), lambda b, i: (b, 0, i)),
            pl.BlockSpec((1, N_ADA, D), lambda b, i: (b, 0, 0)),
            _resident((wa, D), lambda b, i: (0, 0)),
            _resident((wb, D), lambda b, i: (0, 0)),
            _resident((D, dff), lambda b, i: (0, 0)),
            _resident((dff, D), lambda b, i: (0, 0)),
            _resident((1, D), lambda b, i: (0, 0)),
        ],
        out_specs=pl.BlockSpec((1, tm, D), lambda b, i: (b, i, 0)),
        compiler_params=pltpu.CompilerParams(
            dimension_semantics=("parallel", "parallel"), vmem_limit_bytes=_vmem_limit(est)),
        name="mlp",
    )(x, o_gla, o_mobat, mod, w_out_a, w_out_b, w1, w2, g_final)


def _layer(x, mod, w_in, w_gate, b_gate, g_gla_out, tiles, w_out, w_ff1, w_ff2, g_last):
    B, S, D = x.shape
    dv = (D // 2) // GLA_HEADS
    dk = dv // 2
    qk_w, v_w = GLA_HEADS * dk, GLA_HEADS * dv
    hd = (D - v_w) // MOBA_HEADS
    m_w = MOBA_HEADS * hd
    assert 2 * dk == V7X_LANES and dv == V7X_LANES and 2 * hd == V7X_LANES, "head pairs must fill one lane group"
    assert S % MOBA_BLOCK == 0 and S % GLA_CHUNK == 0
    tm = min(ROW_TILE, S)
    assert S % tm == 0

    bounds = np.cumsum([0, qk_w, qk_w, v_w, v_w, GLA_GATE_RANK, m_w, m_w, m_w])
    seg = [w_in[:, bounds[i]:bounds[i + 1]] for i in range(8)]
    gq_w, gk_w, gv_w, gr_w, gg_w, mq_w, mk_w, mv_w = seg
    gg_pad = jnp.pad(gg_w, ((0, 0), (0, V7X_LANES - GLA_GATE_RANK)))
    w_tok = jnp.concatenate([gq_w * dk ** -0.5, gk_w, gv_w, gr_w, mq_w * (hd ** -0.5 * LOG2E), mk_w, gg_pad], axis=1).astype(BF16)
    widths = (qk_w, qk_w, v_w, v_w, m_w, m_w, V7X_LANES)
    w_mvt = mv_w.T.astype(BF16)
    w_gate2 = jnp.pad(w_gate, ((0, V7X_LANES - GLA_GATE_RANK), (0, 0))).astype(BF16)

    gq, gk, ga, gv, gr, mq, mk, mvt = _proj(x, mod, w_tok, widths, w_mvt, w_gate2, b_gate[None, :], tm)
    o_gla = _gla(gq, gk, ga, gv, gr, g_gla_out[None, :], dk, dv)
    o_mobat = _moba(mq, mk, mvt, tiles, hd)
    w_out_b16 = w_out.astype(BF16)
    return _mlp(x, o_gla, o_mobat, mod, w_out_b16[:v_w], w_out_b16[v_w:], w_ff1.astype(BF16), w_ff2.astype(BF16),
                g_last[None, :], tm)


def kernel(x, c, w_ada, b_ada, g_mix, w_in, w_gla_gate, b_gla_gate, g_gla_out, rel_bias, w_out, g_mlp, w_ff1, w_ff2,
           g_final):
    depth = w_ada.shape[0]
    assert depth == 1, "the fused MLP kernel applies the final RMSNorm; one layer is supported"
    D = x.shape[-1]
    tiles = _bias_tiles(rel_bias)
    l = 0
    ones, zeros = jnp.ones((D,), F32), jnp.zeros((D,), F32)
    mul = jnp.stack([ones, g_mix[l], ones, ones, g_mlp[l], ones])[:, None, :]
    add = jnp.stack([zeros, g_mix[l], zeros, zeros, g_mlp[l], zeros])[:, None, :]
    mod = _ada(c, w_ada[l], b_ada[l][None, :], mul, add).reshape(c.shape[0], N_ADA, D)
    return _layer(x, mod, w_in[l], w_gla_gate[l], b_gla_gate[l], g_gla_out[l], tiles, w_out[l], w_ff1[l], w_ff2[l],
                  g_final)
```

```python
import functools
import math

import numpy as np
import jax
import jax.numpy as jnp
from jax import lax
from jax.experimental import pallas as pl
from jax.experimental.pallas import tpu as pltpu

F32, BF16 = jnp.float32, jnp.bfloat16

GLA_HEADS = 4
GLA_GATE_RANK = 16
GLA_TAU = 16.0
GLA_CHUNK = 64
MOBA_HEADS = 8
MOBA_BLOCK = 256
MOBA_TOPK = 3
REL_BUCKETS = 32
REL_MAX_DIST = 128
N_ADA = 6
EPS = 1e-6

V7X_LANES = 128
V7X_VMEM_BYTES = 64 * 2 ** 20
MASKED = -1e30
LOG2E = math.log2(math.e)

PROJ_TILE = 1024
ROW_TILE = 512
FF_CHUNK = 1024


def _vmem_limit(nbytes):
    return int(min(nbytes * 3 // 2, V7X_VMEM_BYTES * 15 // 16))


def _resident(shape, index_map):
    return pl.BlockSpec(shape, index_map, pipeline_mode=pl.Buffered(1))


def _nt(a, b):
    return lax.dot_general(a, b, (((1,), (1,)), ((), ())), preferred_element_type=F32)


def _tn(a, b):
    return lax.dot_general(a, b, (((0,), (0,)), ((), ())), preferred_element_type=F32)


def _split2(x):
    hi = x.astype(BF16)
    lo = (x - hi.astype(F32)).astype(BF16)
    return hi, lo


def _ada_kernel(c_ref, w_ref, b_ref, mul_ref, add_ref, o_ref):
    c = c_ref[...]
    c_act = c * jax.nn.sigmoid(c)
    ada = jnp.dot(c_act, w_ref[...], preferred_element_type=F32, precision=lax.Precision.HIGHEST) + b_ref[...]
    o_ref[...] = ada * mul_ref[0] + add_ref[0]


def _ada(c, w_ada, b_ada, mul, add):
    B, D = c.shape
    return pl.pallas_call(
        _ada_kernel,
        out_shape=jax.ShapeDtypeStruct((B, N_ADA * D), F32),
        grid=(N_ADA,),
        in_specs=[
            pl.BlockSpec((B, D), lambda j: (0, 0)),
            pl.BlockSpec((D, D), lambda j: (0, j)),
            pl.BlockSpec((1, D), lambda j: (0, j)),
            pl.BlockSpec((1, 1, D), lambda j: (j, 0, 0)),
            pl.BlockSpec((1, 1, D), lambda j: (j, 0, 0)),
        ],
        out_specs=pl.BlockSpec((B, D), lambda j: (0, j)),
        compiler_params=pltpu.CompilerParams(
            dimension_semantics=("parallel",), vmem_limit_bytes=_vmem_limit(2 * D * D * 4 + 8 * B * D * 4)),
        name="ada",
    )(c, w_ada, b_ada, mul, add)


def _t5_bucket_np(rel):
    n = np.maximum(rel, 0)
    max_exact = REL_BUCKETS // 2
    ratio = np.maximum(n, 1).astype(np.float64) / max_exact
    large = max_exact + (np.log(ratio) / math.log(REL_MAX_DIST / max_exact) * (REL_BUCKETS - max_exact)).astype(np.int64)
    large = np.minimum(large, REL_BUCKETS - 1)
    return np.where(n < max_exact, n, large).astype(np.int32)


def _bias_kernel(rb_ref, idx_ref, o_ref):
    blk = idx_ref.shape[-1]
    idx = idx_ref[...]
    for hh in range(2):
        h = 2 * pl.program_id(0) + hh
        b_far = rb_ref[REL_BUCKETS - 1, h]
        acc = jnp.full(idx.shape, MASKED, F32)
        for b in range(REL_BUCKETS):
            acc = jnp.where(idx == b, (rb_ref[b, h] - b_far) * LOG2E, acc)
        o_ref[0, :, :, hh * blk:(hh + 1) * blk] = acc


def _bias_tiles(rel_bias):
    blk = MOBA_BLOCK
    k = np.arange(blk)[:, None]
    q = np.arange(blk)[None, :]
    own = np.where(q >= k, _t5_bucket_np(q - k), -1)
    prev = _t5_bucket_np(blk + q - k)
    idx = jnp.asarray(np.stack([own, prev]).astype(np.int32))
    return pl.pallas_call(
        _bias_kernel,
        out_shape=jax.ShapeDtypeStruct((MOBA_HEADS // 2, 2, blk, 2 * blk), F32),
        grid=(MOBA_HEADS // 2,),
        in_specs=[
            pl.BlockSpec(memory_space=pltpu.SMEM),
            pl.BlockSpec((2, blk, blk), lambda p: (0, 0, 0)),
        ],
        out_specs=pl.BlockSpec((1, 2, blk, 2 * blk), lambda p: (p, 0, 0, 0)),
        compiler_params=pltpu.CompilerParams(dimension_semantics=("parallel",)),
        name="bias",
    )(rel_bias, idx)


def _proj_kernel(widths, x_ref, mod_ref, w_ref, wt_ref, wg_ref, bg_ref,
                 gq_ref, gk_ref, ga_ref, gv_ref, gr_ref, mq_ref, mk_ref, mvt_ref):
    x = x_ref[0]
    shift = mod_ref[0, 0:1, :]
    gain = mod_ref[0, 1:2, :]
    r = lax.rsqrt(jnp.mean(x * x, axis=-1, keepdims=True) + EPS)
    hb = ((x * r) * gain + shift).astype(BF16)

    t = _nt(wt_ref[...], hb)
    wmv = mvt_ref.shape[1]
    mvt_ref[0] = t[:wmv].astype(BF16)
    z = _tn(t[wmv:].astype(BF16), wg_ref[...]) + bg_ref[...]
    ga_ref[0] = (jnp.minimum(z, 0.0) - jnp.log1p(jnp.exp(-jnp.abs(z)))) * (1.0 / GLA_TAU)

    outs = (gq_ref, gk_ref, gv_ref, gr_ref, mq_ref, mk_ref)
    off = 0
    for o_ref, wd in zip(outs, widths):
        y = jnp.dot(hb, w_ref[:, off:off + wd], preferred_element_type=F32)
        if o_ref is gr_ref:
            y = y * jax.nn.sigmoid(y)
        o_ref[0] = y.astype(o_ref.dtype)
        off += wd


def _proj(x, mod, w_tok, widths, w_t, w_gate, b_gate2, tm):
    B, S, D = x.shape
    wq, wk, wv, wr, wmq, wmk = widths
    wmv = w_t.shape[0] - w_gate.shape[0]
    wtot = w_tok.shape[1]
    tok = lambda w: pl.BlockSpec((1, tm, w), lambda b, i: (b, i, 0))
    est = (2 * tm * D * 4 + wtot * D * 2 + w_t.shape[0] * D * 2
           + 2 * tm * (3 * wq * 4 + (wv + wr + wmq + wmk + wmv) * 2) + 6 * tm * D * 4)
    return pl.pallas_call(
        functools.partial(_proj_kernel, widths),
        out_shape=(
            jax.ShapeDtypeStruct((B, S, wq), F32), jax.ShapeDtypeStruct((B, S, wk), F32),
            jax.ShapeDtypeStruct((B, S, wk), F32),
            jax.ShapeDtypeStruct((B, S, wv), BF16), jax.ShapeDtypeStruct((B, S, wr), BF16),
            jax.ShapeDtypeStruct((B, S, wmq), BF16), jax.ShapeDtypeStruct((B, S, wmk), BF16),
            jax.ShapeDtypeStruct((B, wmv, S), BF16),
        ),
        grid=(B, S // tm),
        in_specs=[
            pl.BlockSpec((1, tm, D), lambda b, i: (b, i, 0)),
            pl.BlockSpec((1, N_ADA, D), lambda b, i: (b, 0, 0)),
            _resident((D, wtot), lambda b, i: (0, 0)),
            _resident(w_t.shape, lambda b, i: (0, 0)),
            _resident(w_gate.shape, lambda b, i: (0, 0)),
            _resident(b_gate2.shape, lambda b, i: (0, 0)),
        ],
        out_specs=(tok(wq), tok(wk), tok(wk), tok(wv), tok(wr), tok(wmq), tok(wmk),
                   pl.BlockSpec((1, wmv, tm), lambda b, i: (b, 0, i))),
        compiler_params=pltpu.CompilerParams(
            dimension_semantics=("parallel", "parallel"), vmem_limit_bytes=_vmem_limit(est)),
        name="proj",
    )(x, mod, w_tok, w_t, w_gate, b_gate2)


def _gla_kernel(dk, dv, q_ref, k_ref, a_ref, v_ref, r_ref, g_ref, o_ref, st_ref):
    S = q_ref.shape[1]
    C = GLA_CHUNK
    N = S // C
    W = 2 * dk
    q = q_ref[0].reshape(N, C, W)
    k = k_ref[0].reshape(N, C, W)
    g = a_ref[0].reshape(N, C, W)

    row = lax.broadcasted_iota(jnp.int32, (C, C), 0)
    col = lax.broadcasted_iota(jnp.int32, (C, C), 1)
    causal = col <= row
    tri = jnp.broadcast_to(causal.astype(BF16)[None], (N, C, C))
    b = None
    for part in _split2(g):
        t = jnp.einsum('nij,njd->nid', tri, part, preferred_element_type=F32)
        b = t if b is None else b + t
    b_last = b[:, C - 1:C, :]
    b_mid = b[:, C // 2 - 1:C // 2, :]

    qd = q * jnp.exp(b - b_mid)
    kd = (k * jnp.exp(b_mid - b)).astype(BF16)
    kdec = (k * jnp.exp(b_last - b)).astype(BF16)
    qb = q * jnp.exp(b)
    decay = jnp.exp(b_last)

    lane = lax.broadcasted_iota(jnp.int32, (1, 1, W), 2)
    head_lanes = (lane < dk, lane >= dk)
    vs = [v_ref[0, :, hh * dv:(hh + 1) * dv].reshape(N, C, dv) for hh in range(2)]

    kvt = [jnp.einsum('ncv,ncd->nvd', vs[hh], kdec, preferred_element_type=F32) for hh in range(2)]
    kvt = jnp.where(head_lanes[0], kvt[0], kvt[1])
    state = jnp.zeros((dv, W), F32)
    for n in range(N):
        st_ref[n] = state.astype(BF16)
        state = state * decay[n] + kvt[n]

    for hh in range(2):
        qd_h = jnp.where(head_lanes[hh], qd, 0.0).astype(BF16)
        qb_h = jnp.where(head_lanes[hh], qb, 0.0).astype(BF16)
        att = jnp.einsum('nid,njd->nij', qd_h, kd, preferred_element_type=F32)
        att = jnp.where(causal[None], att, 0.0).astype(BF16)
        o = jnp.einsum('nij,njv->niv', att, vs[hh], preferred_element_type=F32)
        o = o + jnp.einsum('ncd,nvd->ncv', qb_h, st_ref[...], preferred_element_type=F32)
        o = o.reshape(S, dv)
        rr = lax.rsqrt(jnp.mean(o * o, axis=-1, keepdims=True) + EPS)
        gate = r_ref[0, :, hh * dv:(hh + 1) * dv].astype(F32)
        o_ref[0, :, hh * dv:(hh + 1) * dv] = ((o * rr) * g_ref[:, hh * dv:(hh + 1) * dv] * gate).astype(BF16)


def _gla(gq, gk, ga, gv, gr, g_out, dk, dv):
    B, S, _ = gq.shape
    pairs = GLA_HEADS // 2
    W = 2 * dk
    N = S // GLA_CHUNK
    qspec = pl.BlockSpec((1, S, W), lambda b, p: (b, 0, p))
    vspec = pl.BlockSpec((1, S, 2 * dv), lambda b, p: (b, 0, p))
    est = 2 * (3 * S * W * 4 + 3 * S * 2 * dv * 2) + 10 * S * W * 4 + 4 * N * dv * W * 4 + 4 * S * dv * 4
    return pl.pallas_call(
        functools.partial(_gla_kernel, dk, dv),
        out_shape=jax.ShapeDtypeStruct((B, S, GLA_HEADS * dv), BF16),
        grid=(B, pairs),
        in_specs=[qspec, qspec, qspec, vspec, vspec, pl.BlockSpec((1, 2 * dv), lambda b, p: (0, p))],
        out_specs=vspec,
        scratch_shapes=[pltpu.VMEM((N, dv, W), BF16)],
        compiler_params=pltpu.CompilerParams(
            dimension_semantics=("parallel", "parallel"), vmem_limit_bytes=_vmem_limit(est)),
        name="gla",
    )(gq, gk, ga, gv, gr, g_out)


def _moba_kernel(hd, q_ref, k_ref, vt_ref, t_ref, o_ref, s_ref, p_ref, ve_ref):
    S = q_ref.shape[1]
    BLK = MOBA_BLOCK
    NB = S // BLK
    W = 2 * hd
    VR = ve_ref.shape[0] // 2
    lane = lax.broadcasted_iota(jnp.int32, (1, W), 1)
    head_lanes = (lane < hd, lane >= hd)
    zero = jnp.zeros((), BF16)
    k2 = k_ref[0]
    q2 = q_ref[0]

    for h in range(2):
        ve_ref[h * VR:h * VR + hd, :] = vt_ref[0, h * hd:(h + 1) * hd, :]
        ve_ref[h * VR + hd:(h + 1) * VR, :] = jnp.ones((VR - hd, S), BF16)

    first = MOBA_TOPK + 1
    masks = None
    if NB > first:
        lo = first * BLK
        kmean = jnp.sum(k2.astype(F32).reshape(NB, BLK, W), axis=1) * (1.0 / BLK)
        parts = []
        for h in range(2):
            km = jnp.where(head_lanes[h], kmean, 0.0)
            km_hi = km.astype(BF16).astype(F32)
            parts += [km_hi, km - km_hi]
        g4 = _nt(jnp.concatenate(parts, axis=0).astype(BF16), q2[lo:])
        blk = lax.broadcasted_iota(jnp.int32, (NB, S - lo), 0)
        own = lax.broadcasted_iota(jnp.int32, (NB, S - lo), 1) // BLK + first
        masks = []
        for h in range(2):
            gate = g4[2 * h * NB:(2 * h + 1) * NB] + g4[(2 * h + 1) * NB:(2 * h + 2) * NB]
            rank = jnp.zeros(gate.shape, jnp.int32)
            for r in range(1, NB):
                rival = pltpu.roll(gate, r, axis=0)
                rblk = jnp.where(blk < r, blk + (NB - r), blk - r)
                beats = (rblk < own) & ((rival > gate) | ((rival == gate) & (rblk < blk)))
                rank = rank + beats.astype(jnp.int32)
            chosen = (blk < own) & (rank < MOBA_TOPK)
            masks.append(jnp.where(chosen, 0.0, MASKED))

    def mrow(i, j):
        if j == i or i < first:
            return None
        cols = slice((i - first) * BLK, (i - first + 1) * BLK)
        return jnp.concatenate([masks[h][j:j + 1, cols] for h in range(2)], axis=1)

    qtile = {}

    def scores(i, j):
        if i not in qtile:
            qi = q2[i * BLK:(i + 1) * BLK]
            qtile[i] = jnp.concatenate([jnp.where(head_lanes[h], qi, zero) for h in range(2)], axis=0)
        s = _nt(k2[j * BLK:(j + 1) * BLK], qtile[i])
        if j == i:
            s = s + t_ref[0, 0]
        elif j == i - 1:
            s = s + t_ref[0, 1]
        s_ref[i % 2, j * BLK:(j + 1) * BLK, :] = s
        mb = jnp.max(s, axis=0, keepdims=True)
        m = mrow(i, j)
        return mb if m is None else mb + m

    def weights(i, j, m_run):
        m = mrow(i, j)
        c = m_run if m is None else m_run - m
        rows = slice(j * BLK, (j + 1) * BLK)
        p_ref[i % 2, rows, :] = jnp.exp2(s_ref[i % 2, rows, :] - c).astype(BF16)

    col_max = {0: [scores(0, 0)]}
    for i in range(NB):
        m_run = functools.reduce(jnp.maximum, col_max.pop(i))
        nxt = list(range(i + 1, -1, -1)) if i + 1 < NB else []
        col_max[i + 1] = []
        for j in range(i + 1):
            weights(i, j, m_run)
            while nxt and len(col_max[i + 1]) * (i + 1) < (j + 1) * (i + 2):
                col_max[i + 1].append(scores(i + 1, nxt.pop(0)))
        nk = (i + 1) * BLK
        for h in range(2):
            ot = jnp.dot(ve_ref[h * VR:(h + 1) * VR, :nk], p_ref[i % 2, :nk, h * BLK:(h + 1) * BLK],
                         preferred_element_type=F32)
            o_ref[0, h * hd:(h + 1) * hd, i * BLK:(i + 1) * BLK] = (ot[:hd] / ot[hd:hd + 1]).astype(BF16)


def _moba(mq, mk, mvt, tiles, hd):
    B, S, _ = mq.shape
    W = 2 * hd
    BLK = MOBA_BLOCK
    assert S // BLK <= 8, "block ranking keeps the candidate blocks on the sublanes of one vector register"
    ve_rows = 2 * (hd + 16)
    est = (2 * (2 * S * W * 2 + 2 * W * S * 2 + 4 * BLK * BLK * 4) + 2 * S * 2 * BLK * 6 + ve_rows * S * 2
           + 3 * S * W * 4 + 32 * BLK * BLK * 4)
    return pl.pallas_call(
        functools.partial(_moba_kernel, hd),
        out_shape=jax.ShapeDtypeStruct((B, MOBA_HEADS * hd, S), BF16),
        grid=(B, MOBA_HEADS // 2),
        in_specs=[
            pl.BlockSpec((1, S, W), lambda b, p: (b, 0, p)),
            pl.BlockSpec((1, S, W), lambda b, p: (b, 0, p)),
            pl.BlockSpec((1, W, S), lambda b, p: (b, p, 0)),
            pl.BlockSpec((1, 2, BLK, 2 * BLK), lambda b, p: (p, 0, 0, 0)),
        ],
        out_specs=pl.BlockSpec((1, W, S), lambda b, p: (b, p, 0)),
        scratch_shapes=[pltpu.VMEM((2, S, 2 * BLK), F32), pltpu.VMEM((2, S, 2 * BLK), BF16),
                        pltpu.VMEM((ve_rows, S), BF16)],
        compiler_params=pltpu.CompilerParams(
            dimension_semantics=("parallel", "parallel"), vmem_limit_bytes=_vmem_limit(est)),
        name="moba",
    )(mq, mk, mvt, tiles)


def _mlp_kernel(x_ref, og_ref, omt_ref, mod_ref, woa_ref, wob_ref, w1_ref, w2_ref, gf_ref, o_ref):
    gate_a = mod_ref[0, 2:3, :]
    shift_m = mod_ref[0, 3:4, :]
    gain_m = mod_ref[0, 4:5, :]
    gate_m = mod_ref[0, 5:6, :]
    y = jnp.dot(og_ref[0], woa_ref[...], preferred_element_type=F32) + _tn(omt_ref[0], wob_ref[...])
    x1 = x_ref[0] + gate_a * y
    r = lax.rsqrt(jnp.mean(x1 * x1, axis=-1, keepdims=True) + EPS)
    hb = ((x1 * r) * gain_m + shift_m).astype(BF16)
    ff = None
    for c0 in range(0, w1_ref.shape[1], FF_CHUNK):
        u = jnp.dot(hb, w1_ref[:, c0:c0 + FF_CHUNK], preferred_element_type=F32)
        u = jnp.maximum(u, 0.0)
        t = jnp.dot((u * u).astype(BF16), w2_ref[c0:c0 + FF_CHUNK, :], preferred_element_type=F32)
        ff = t if ff is None else ff + t
    x2 = x1 + gate_m * ff
    r2 = lax.rsqrt(jnp.mean(x2 * x2, axis=-1, keepdims=True) + EPS)
    o_ref[0] = (x2 * r2) * gf_ref[...]


def _mlp(x, o_gla, o_mobat, mod, w_out_a, w_out_b, w1, w2, g_final, tm):
    B, S, D = x.shape
    wa, wb, dff = w_out_a.shape[0], w_out_b.shape[0], w1.shape[1]
    est = ((wa + wb) * D * 2 + 2 * D * dff * 2 + 4 * tm * D * 4 + 2 * tm * (wa + wb) * 2
           + 6 * tm * D * 4 + 2 * tm * FF_CHUNK * 4)
    return pl.pallas_call(
        _mlp_kernel,
        out_shape=jax.ShapeDtypeStruct((B, S, D), x.dtype),
        grid=(B, S // tm),
        in_specs=[
            pl.BlockSpec((1, tm, D), lambda b, i: (b, i, 0)),
            pl.BlockSpec((1, tm, wa), lambda b, i: (b, i, 0)),
            pl.BlockSpec((1, wb, tm), lambda b, i: (b, 0, i)),
            pl.BlockSpec((1, N_ADA, D), lambda b, i: (b, 0, 0)),
            _resident((wa, D), lambda b, i: (0, 0)),
            _resident((wb, D), lambda b, i: (0, 0)),
            _resident((D, dff), lambda b, i: (0, 0)),
            _resident((dff, D), lambda b, i: (0, 0)),
            _resident((1, D), lambda b, i: (0, 0)),
        ],
        out_specs=pl.BlockSpec((1, tm, D), lambda b, i: (b, i, 0)),
        compiler_params=pltpu.CompilerParams(
            dimension_semantics=("parallel", "parallel"), vmem_limit_bytes=_vmem_limit(est)),
        name="mlp",
    )(x, o_gla, o_mobat, mod, w_out_a, w_out_b, w1, w2, g_final)


def _layer(x, mod, w_in, w_gate, b_gate, g_gla_out, tiles, w_out, w_ff1, w_ff2, g_last):
    B, S, D = x.shape
    dv = (D // 2) // GLA_HEADS
    dk = dv // 2
    qk_w, v_w = GLA_HEADS * dk, GLA_HEADS * dv
    hd = (D - v_w) // MOBA_HEADS
    m_w = MOBA_HEADS * hd
    assert 2 * dk == V7X_LANES and dv == V7X_LANES and 2 * hd == V7X_LANES, "head pairs must fill one lane group"
    assert S % MOBA_BLOCK == 0 and S % GLA_CHUNK == 0
    tm = min(ROW_TILE, S)
    tp = min(PROJ_TILE, S)
    assert S % tm == 0 and S % tp == 0

    bounds = np.cumsum([0, qk_w, qk_w, v_w, v_w, GLA_GATE_RANK, m_w, m_w, m_w])
    seg = [w_in[:, bounds[i]:bounds[i + 1]] for i in range(8)]
    gq_w, gk_w, gv_w, gr_w, gg_w, mq_w, mk_w, mv_w = seg
    w_tok = jnp.concatenate([gq_w * dk ** -0.5, gk_w, gv_w, gr_w, mq_w * (hd ** -0.5 * LOG2E), mk_w], axis=1).astype(BF16)
    widths = (qk_w, qk_w, v_w, v_w, m_w, m_w)
    w_t = jnp.concatenate([mv_w, gg_w], axis=1).T.astype(BF16)

    gq, gk, ga, gv, gr, mq, mk, mvt = _proj(x, mod, w_tok, widths, w_t, w_gate.astype(BF16), b_gate[None, :], tp)
    o_gla = _gla(gq, gk, ga, gv, gr, g_gla_out[None, :], dk, dv)
    o_mobat = _moba(mq, mk, mvt, tiles, hd)
    w_out_b16 = w_out.astype(BF16)
    return _mlp(x, o_gla, o_mobat, mod, w_out_b16[:v_w], w_out_b16[v_w:], w_ff1.astype(BF16), w_ff2.astype(BF16),
                g_last[None, :], tm)


def kernel(x, c, w_ada, b_ada, g_mix, w_in, w_gla_gate, b_gla_gate, g_gla_out, rel_bias, w_out, g_mlp, w_ff1, w_ff2,
           g_final):
    depth = w_ada.shape[0]
    assert depth == 1, "the fused MLP kernel applies the final RMSNorm; one layer is supported"
    D = x.shape[-1]
    tiles = _bias_tiles(rel_bias)
    l = 0
    ones, zeros = jnp.ones((D,), F32), jnp.zeros((D,), F32)
    mul = jnp.stack([ones, g_mix[l], ones, ones, g_mlp[l], ones])[:, None, :]
    add = jnp.stack([zeros, g_mix[l], zeros, zeros, g_mlp[l], zeros])[:, None, :]
    mod = _ada(c, w_ada[l], b_ada[l][None, :], mul, add).reshape(c.shape[0], N_ADA, D)
    return _layer(x, mod, w_in[l], w_gla_gate[l], b_gla_gate[l], g_gla_out[l], tiles, w_out[l], w_ff1[l], w_ff2[l],
                  g_final)
```
